```python
import math
import jax, jax.numpy as jnp
from jax import lax
import numpy as np

D_MODEL = 4096
BATCH = 2
SEQ = 4096
DEPTH = 2

N_MIXERS = 2
N_ATTN = (DEPTH + 1) // 2
N_SSM = DEPTH // 2

N_HEADS = 64
N_KV = 8
GROUP = N_HEADS // N_KV
HEAD_DIM = 64
Q_W = N_HEADS * HEAD_DIM
KV_W = N_KV * HEAD_DIM
WINDOW = 128
BLOCK = 128

NUM_BUCKETS = 32
MAX_EXACT = NUM_BUCKETS // 2
MAX_DISTANCE = 128

SSM_GC = 16
SSM_G = D_MODEL // SSM_GC
SSM_P = 64
DT_MIN = 1e-3
DT_MAX = 1e-1

D_FF = 11008
CONV_W = 3

EPS = 1e-6
NEG = -1e30

kernel_name = "hybrid_swa_sink_s5_convffn"


def rmsnorm(x, g):
    x32 = x.astype(jnp.float32)
    y = x32 * lax.rsqrt(jnp.mean(x32 * x32, axis=-1, keepdims=True) + EPS)
    return (y * g.astype(jnp.float32)).astype(x.dtype)


def t5_buckets(dist):
    n = np.maximum(dist, 0)
    is_small = n < MAX_EXACT
    large = MAX_EXACT + (np.log(np.maximum(n, 1) / MAX_EXACT) / np.log(MAX_DISTANCE / MAX_EXACT)
                         * (NUM_BUCKETS - MAX_EXACT)).astype(np.int32)
    large = np.minimum(large, NUM_BUCKETS - 1)
    return np.where(is_small, n, large).astype(np.int32)


def swa_sink_attention(h, w_qkv, w_o, sinks, rel_bias):
    b, L, _ = h.shape
    nb = L // BLOCK
    dt = h.dtype
    qkv = h @ w_qkv
    q = qkv[..., :Q_W].reshape(b, nb, BLOCK, N_KV, GROUP, HEAD_DIM)
    k = qkv[..., Q_W:Q_W + KV_W].reshape(b, nb, BLOCK, N_KV, HEAD_DIM)
    v = qkv[..., Q_W + KV_W:].reshape(b, nb, BLOCK, N_KV, HEAD_DIM)
    pad = ((0, 0), (1, 0), (0, 0), (0, 0), (0, 0))
    kb = jnp.concatenate([jnp.pad(k, pad)[:, :-1], k], axis=2)
    vb = jnp.concatenate([jnp.pad(v, pad)[:, :-1], v], axis=2)
    s = jnp.einsum('bnqkgd,bnskd->bnkgqs', q, kb).astype(jnp.float32) * (HEAD_DIM ** -0.5)
    qi = np.arange(BLOCK)[:, None] + BLOCK
    kj = np.arange(2 * BLOCK)[None, :]
    dist = qi - kj
    local = (dist >= 0) & (dist < WINDOW)
    bias = rel_bias.astype(jnp.float32)[t5_buckets(dist)]
    bias = jnp.transpose(bias, (2, 0, 1)).reshape(N_KV, GROUP, BLOCK, 2 * BLOCK)
    blk = jnp.arange(nb)[:, None, None]
    valid = jnp.asarray(local)[None] & ((blk > 0) | jnp.asarray(kj >= BLOCK)[None])
    s = jnp.where(valid[None, :, None, None], s + bias, NEG)
    sink = sinks.astype(jnp.float32).reshape(1, 1, N_KV, GROUP, 1, 1)
    m = jnp.maximum(jnp.max(s, axis=-1, keepdims=True), sink)
    p = jnp.exp(s - m)
    p = p / (jnp.sum(p, axis=-1, keepdims=True) + jnp.exp(sink - m))
    o = jnp.einsum('bnkgqs,bnskd->bnqkgd', p.astype(dt), vb)
    return o.reshape(b, L, Q_W) @ w_o


def s5_scan_combine(e1, e2):
    ar1, ai1, br1, bi1 = e1
    ar2, ai2, br2, bi2 = e2
    ar = ar2 * ar1 - ai2 * ai1
    ai = ar2 * ai1 + ai2 * ar1
    br = ar2 * br1 - ai2 * bi1 + br2
    bi = ar2 * bi1 + ai2 * br1 + bi2
    return (ar, ai, br, bi)


def s5_layer(h, lam_re, lam_im, log_step, b_re, b_im, c_re, c_im, d_skip, w_glu):
    b, L, D = h.shape
    dt = h.dtype
    delta = jnp.exp(log_step.astype(jnp.float32))[:, None]
    lr = lam_re.astype(jnp.float32)
    li = lam_im.astype(jnp.float32)
    mag = jnp.exp(lr * delta)
    abar_r = mag * jnp.cos(li * delta)
    abar_i = mag * jnp.sin(li * delta)
    nr = abar_r - 1.0
    ni = abar_i
    den = lr * lr + li * li
    fr = ((nr * lr + ni * li) / den)[..., None]
    fi = ((ni * lr - nr * li) / den)[..., None]
    br32 = b_re.astype(jnp.float32)
    bi32 = b_im.astype(jnp.float32)
    bbar_r = (fr * br32 - fi * bi32).astype(dt)
    bbar_i = (fr * bi32 + fi * br32).astype(dt)
    u = h.reshape(b, L, SSM_G, SSM_GC)
    bu_r = jnp.einsum('blgc,gpc->blgp', u, bbar_r)
    bu_i = jnp.einsum('blgc,gpc->blgp', u, bbar_i)
    a_r = jnp.broadcast_to(abar_r.astype(dt)[None, None], (1, L, SSM_G, SSM_P))
    a_i = jnp.broadcast_to(abar_i.astype(dt)[None, None], (1, L, SSM_G, SSM_P))
    _, _, xr, xi = lax.associative_scan(s5_scan_combine, (a_r, a_i, bu_r, bu_i), axis=1)
    y = jnp.einsum('blgp,gcp->blgc', xr, c_re) - jnp.einsum('blgp,gcp->blgc', xi, c_im)
    y = y.reshape(b, L, D) + d_skip * h
    z = jax.nn.gelu(y) @ w_glu
    return z[..., :D] * jax.nn.sigmoid(z[..., D:])


def conv_gated_mlp(h, w_up, conv_w, conv_b, w_down):
    L = h.shape[1]
    u = h @ w_up
    up = jnp.pad(u, ((0, 0), (CONV_W - 1, 0), (0, 0)))
    c = conv_b + sum(conv_w[j] * up[:, j:j + L] for j in range(CONV_W))
    g = c[..., :D_FF]
    v = c[..., D_FF:]
    return (jax.nn.silu(g) * v) @ w_down


def setup_inputs(seed: int = 0) -> dict:
    key = jax.random.key(seed)
    ks = jax.random.split(key, 24)
    f32 = jnp.float32
    nrm = lambda k, shape, scale: jax.random.normal(k, shape, f32) * scale
    x = jax.random.normal(ks[0], (BATCH, SEQ, D_MODEL), f32)
    attn_norm = 1.0 + nrm(ks[1], (N_ATTN, D_MODEL), 0.02)
    w_qkv = nrm(ks[2], (N_ATTN, D_MODEL, Q_W + 2 * KV_W), D_MODEL ** -0.5)
    w_o = nrm(ks[3], (N_ATTN, Q_W, D_MODEL), Q_W ** -0.5)
    sinks = nrm(ks[4], (N_ATTN, N_HEADS), 0.5)
    rel_bias = nrm(ks[5], (NUM_BUCKETS, N_HEADS), 0.5)
    ssm_norm = 1.0 + nrm(ks[6], (N_SSM, D_MODEL), 0.02)
    n_idx = jnp.arange(SSM_P, dtype=f32)
    lambda_re = -0.5 + nrm(ks[7], (N_SSM, SSM_G, SSM_P), 0.01)
    lambda_im = math.pi * n_idx + nrm(ks[8], (N_SSM, SSM_G, SSM_P), 0.01)
    log_step = jax.random.uniform(ks[9], (N_SSM, SSM_G), f32, math.log(DT_MIN), math.log(DT_MAX))
    b_re = nrm(ks[10], (N_SSM, SSM_G, SSM_P, SSM_GC), (2 * SSM_GC) ** -0.5)
    b_im = nrm(ks[11], (N_SSM, SSM_G, SSM_P, SSM_GC), (2 * SSM_GC) ** -0.5)
    c_re = nrm(ks[12], (N_SSM, SSM_G, SSM_GC, SSM_P), (2 * SSM_P) ** -0.5)
    c_im = nrm(ks[13], (N_SSM, SSM_G, SSM_GC, SSM_P), (2 * SSM_P) ** -0.5)
    d_skip = nrm(ks[14], (N_SSM, D_MODEL), 1.0)
    w_glu = nrm(ks[15], (N_SSM, D_MODEL, 2 * D_MODEL), D_MODEL ** -0.5)
    ffn_norm = 1.0 + nrm(ks[16], (DEPTH, D_MODEL), 0.02)
    w_up = nrm(ks[17], (DEPTH, D_MODEL, 2 * D_FF), D_MODEL ** -0.5)
    conv_w = nrm(ks[18], (DEPTH, CONV_W, 2 * D_FF), CONV_W ** -0.5)
    conv_b = nrm(ks[19], (DEPTH, 2 * D_FF), 0.02)
    w_down = nrm(ks[20], (DEPTH, D_FF, D_MODEL), D_FF ** -0.5)
    final_norm = 1.0 + nrm(ks[21], (D_MODEL,), 0.02)
    return {'x': x, 'attn_norm': attn_norm, 'w_qkv': w_qkv, 'w_o': w_o, 'sinks': sinks,
            'rel_bias': rel_bias, 'ssm_norm': ssm_norm, 'lambda_re': lambda_re,
            'lambda_im': lambda_im, 'log_step': log_step, 'b_re': b_re, 'b_im': b_im,
            'c_re': c_re, 'c_im': c_im, 'd_skip': d_skip, 'w_glu': w_glu,
            'ffn_norm': ffn_norm, 'w_up': w_up, 'conv_w': conv_w, 'conv_b': conv_b,
            'w_down': w_down, 'final_norm': final_norm}


def reference(x, attn_norm, w_qkv, w_o, sinks, rel_bias, ssm_norm, lambda_re, lambda_im,
              log_step, b_re, b_im, c_re, c_im, d_skip, w_glu, ffn_norm, w_up, conv_w,
              conv_b, w_down, final_norm):
    for i in range(DEPTH):
        j = i // N_MIXERS
        if i % N_MIXERS == 0:
            x = x + swa_sink_attention(rmsnorm(x, attn_norm[j]), w_qkv[j], w_o[j], sinks[j], rel_bias)
        else:
            x = x + s5_layer(rmsnorm(x, ssm_norm[j]), lambda_re[j], lambda_im[j], log_step[j],
                             b_re[j], b_im[j], c_re[j], c_im[j], d_skip[j], w_glu[j])
        x = x + conv_gated_mlp(rmsnorm(x, ffn_norm[i]), w_up[i], conv_w[i], conv_b[i], w_down[i])
    return rmsnorm(x, final_norm)
```

```python
import functools
import math

import numpy as np
import jax
import jax.numpy as jnp
from jax import lax
from jax.experimental import pallas as pl
from jax.experimental.pallas import tpu as pltpu

F32 = jnp.float32
BF16 = jnp.bfloat16

D_MODEL = 4096
SEQ = 4096
N_HEADS = 64
N_KV = 8
GROUP = N_HEADS // N_KV
HEAD_DIM = 64
Q_W = N_HEADS * HEAD_DIM
KV_W = N_KV * HEAD_DIM
WINDOW = 128
BLOCK = 128
NUM_BUCKETS = 32
MAX_EXACT = NUM_BUCKETS // 2
MAX_DISTANCE = 128
SSM_GC = 16
SSM_P = 64
D_FF = 11008
CONV_W = 3
EPS = 1e-6
NEG = -1e30

LANES = 128
SUBLANES = 8
VMEM_LIMIT_BYTES = 56 * 1024 * 1024

GROUPS_PER_CTILE = LANES // SSM_GC
STATES_PER_CTILE = GROUPS_PER_CTILE * SSM_P


def _cparams(n_axes):
    return pltpu.CompilerParams(dimension_semantics=("arbitrary",) * n_axes,
                                vmem_limit_bytes=VMEM_LIMIT_BYTES)


def _rmsnorm_kernel(x_ref, g_ref, o_ref):
    x = x_ref[...]
    y = x * lax.rsqrt(jnp.mean(x * x, axis=-1, keepdims=True) + EPS)
    o_ref[...] = (y * g_ref[...]).astype(o_ref.dtype)


def rmsnorm(x, g, out_dtype, tr=512):
    n, d = x.shape
    return pl.pallas_call(
        _rmsnorm_kernel,
        grid=(n // tr,),
        in_specs=[pl.BlockSpec((tr, d), lambda i: (i, 0)),
                  pl.BlockSpec((1, d), lambda i: (0, 0))],
        out_specs=pl.BlockSpec((tr, d), lambda i: (i, 0)),
        out_shape=jax.ShapeDtypeStruct((n, d), out_dtype),
        compiler_params=_cparams(1),
        name="rmsnorm",
    )(x, g.reshape(1, d))


def _rowscale_kernel(x_ref, o_ref):
    x = x_ref[...]
    o_ref[...] = lax.rsqrt(jnp.mean(x * x, axis=-1, keepdims=True) + EPS)


def rms_rowscale(x, tr=512):
    n, d = x.shape
    return pl.pallas_call(
        _rowscale_kernel,
        grid=(n // tr,),
        in_specs=[pl.BlockSpec((tr, d), lambda i: (i, 0))],
        out_specs=pl.BlockSpec((tr, 1), lambda i: (i, 0)),
        out_shape=jax.ShapeDtypeStruct((n, 1), F32),
        compiler_params=_cparams(1),
        name="rms_rowscale",
    )(x)


def _mm_kernel(a_ref, w_ref, *rest, has_resid):
    if has_resid:
        r_ref, o_ref, wb_ref = rest
    else:
        o_ref, wb_ref = rest

    @pl.when(pl.program_id(1) == 0)
    def _():
        wb_ref[...] = w_ref[...].astype(BF16)

    acc = jnp.dot(a_ref[...], wb_ref[...], preferred_element_type=F32)
    if has_resid:
        acc = r_ref[...] + acc
    o_ref[...] = acc.astype(o_ref.dtype)


def matmul(a, w, *, resid=None, out_dtype, tm, tn, k_block=0, k_size=None):
    m = a.shape[0]
    n = w.shape[1]
    k_size = a.shape[1] if k_size is None else k_size
    in_specs = [pl.BlockSpec((tm, k_size), lambda j, i: (i, k_block)),
                pl.BlockSpec((k_size, tn), lambda j, i: (k_block, j))]
    args = [a, w]
    if resid is not None:
        in_specs.append(pl.BlockSpec((tm, tn), lambda j, i: (i, j)))
        args.append(resid)
    return pl.pallas_call(
        functools.partial(_mm_kernel, has_resid=resid is not None),
        grid=(n // tn, m // tm),
        in_specs=in_specs,
        out_specs=pl.BlockSpec((tm, tn), lambda j, i: (i, j)),
        out_shape=jax.ShapeDtypeStruct((m, n), out_dtype),
        scratch_shapes=[pltpu.VMEM((k_size, tn), BF16)],
        compiler_params=_cparams(2),
        name="matmul",
    )(*args)


def _glu_mm_kernel(a_ref, w1_ref, w2_ref, r_ref, o_ref, wb_ref, *, tn):
    @pl.when(pl.program_id(1) == 0)
    def _():
        wb_ref[:, :tn] = w1_ref[...].astype(BF16)
        wb_ref[:, tn:] = w2_ref[...].astype(BF16)

    z = jnp.dot(a_ref[...], wb_ref[...], preferred_element_type=F32)
    o_ref[...] = r_ref[...] + z[:, :tn] * jax.nn.sigmoid(z[:, tn:])


def glu_matmul(a, w, resid, *, tm, tn):
    m, k = a.shape
    d = w.shape[1] // 2
    nt = d // tn
    return pl.pallas_call(
        functools.partial(_glu_mm_kernel, tn=tn),
        grid=(nt, m // tm),
        in_specs=[pl.BlockSpec((tm, k), lambda j, i: (i, 0)),
                  pl.BlockSpec((k, tn), lambda j, i: (0, j)),
                  pl.BlockSpec((k, tn), lambda j, i: (0, j + nt)),
                  pl.BlockSpec((tm, tn), lambda j, i: (i, j))],
        out_specs=pl.BlockSpec((tm, tn), lambda j, i: (i, j)),
        out_shape=jax.ShapeDtypeStruct((m, d), F32),
        scratch_shapes=[pltpu.VMEM((k, 2 * tn), BF16)],
        compiler_params=_cparams(2),
        name="glu_matmul",
    )(a, w, w, resid)


def _up_conv_gate_kernel(a_ref, wg_ref, wv_ref, cwg_ref, cwv_ref, cbg_ref, cbv_ref, o_ref,
                         wb_ref, ubuf_ref, *, tm, tn, tiles_per_seq):
    i = pl.program_id(1)

    @pl.when(i == 0)
    def _():
        wb_ref[:, :tn] = wg_ref[...].astype(BF16)
        wb_ref[:, tn:] = wv_ref[...].astype(BF16)

    @pl.when(i % tiles_per_seq == 0)
    def _():
        ubuf_ref[0:SUBLANES, :] = jnp.zeros((SUBLANES, 2 * tn), F32)

    u = jnp.dot(a_ref[...], wb_ref[...], preferred_element_type=F32)
    ubuf_ref[SUBLANES:SUBLANES + tm, :] = u
    u1 = ubuf_ref[SUBLANES - 1:SUBLANES - 1 + tm, :]
    u2 = ubuf_ref[SUBLANES - 2:SUBLANES - 2 + tm, :]
    cw = jnp.concatenate([cwg_ref[...], cwv_ref[...]], axis=-1)
    cb = jnp.concatenate([cbg_ref[...], cbv_ref[...]], axis=-1)
    s = cw[0:1] * u2
    s = s + cw[1:2] * u1
    s = s + cw[2:3] * u
    c = cb + s
    g = c[:, :tn]
    o_ref[...] = (g * jax.nn.sigmoid(g) * c[:, tn:]).astype(o_ref.dtype)
    ubuf_ref[0:SUBLANES, :] = ubuf_ref[tm:tm + SUBLANES, :]


def up_conv_gate(a, w_up, conv_w, conv_b, *, seq, tm, tn):
    m, k = a.shape
    f = w_up.shape[1] // 2
    nt = f // tn
    cb = conv_b.reshape(1, 2 * f)
    return pl.pallas_call(
        functools.partial(_up_conv_gate_kernel, tm=tm, tn=tn, tiles_per_seq=seq // tm),
        grid=(nt, m // tm),
        in_specs=[pl.BlockSpec((tm, k), lambda j, i: (i, 0)),
                  pl.BlockSpec((k, tn), lambda j, i: (0, j)),
                  pl.BlockSpec((k, tn), lambda j, i: (0, j + nt)),
                  pl.BlockSpec((CONV_W, tn), lambda j, i: (0, j)),
                  pl.BlockSpec((CONV_W, tn), lambda j, i: (0, j + nt)),
                  pl.BlockSpec((1, tn), lambda j, i: (0, j)),
                  pl.BlockSpec((1, tn), lambda j, i: (0, j + nt))],
        out_specs=pl.BlockSpec((tm, tn), lambda j, i: (i, j)),
        out_shape=jax.ShapeDtypeStruct((m, f), BF16),
        scratch_shapes=[pltpu.VMEM((k, 2 * tn), BF16),
                        pltpu.VMEM((tm + SUBLANES, 2 * tn), F32)],
        compiler_params=_cparams(2),
        name="up_conv_gate",
    )(a, w_up, w_up, conv_w, conv_w, cb, cb)


def _t5_bucket_table():
    qi = np.arange(BLOCK)[:, None] + BLOCK
    kj = np.arange(2 * BLOCK)[None, :]
    n = np.maximum(qi - kj, 0)
    is_small = n < MAX_EXACT
    large = MAX_EXACT + (np.log(np.maximum(n, 1) / MAX_EXACT) / np.log(MAX_DISTANCE / MAX_EXACT)
                         * (NUM_BUCKETS - MAX_EXACT)).astype(np.int32)
    large = np.minimum(large, NUM_BUCKETS - 1)
    return np.where(is_small, n, large).astype(np.int32)


def _split_kv_pair(x):
    u = pltpu.bitcast(x, jnp.uint32)
    ru = pltpu.roll(u, LANES // 2, 1)
    lo = lax.broadcasted_iota(jnp.int32, u.shape, 1) < LANES // 2
    zero = jnp.zeros(u.shape, jnp.uint32)
    as_bf16 = lambda v: pltpu.bitcast(v, BF16)
    even = (as_bf16(jnp.where(lo, u, zero)), as_bf16(jnp.where(lo, zero, ru)))
    odd = (as_bf16(jnp.where(lo, ru, zero)), as_bf16(jnp.where(lo, zero, u)))
    return even, odd


def _attn_kernel(relb_ref, sink_ref, bucket_ref, q_ref, kc_ref, kp_ref, vc_ref, vp_ref, o_ref,
                 bias_ref, *, blocks_per_seq):
    r = pl.program_id(0)
    pairs = GROUP // 2
    qrows = pairs * BLOCK

    @pl.when(r == 0)
    def _():
        bucket = bucket_ref[...]

        def head_body(h, carry):
            tile = jnp.zeros((BLOCK, 2 * BLOCK), F32)
            for b in range(NUM_BUCKETS):
                tile = jnp.where(bucket == b, relb_ref[b, h], tile)
            g = h // GROUP
            a = (h % GROUP) // 2
            par = h % 2
            bias_ref[2 * g + par, pl.ds(pl.multiple_of(a * BLOCK, BLOCK), BLOCK), :] = tile
            return carry

        lax.fori_loop(0, N_HEADS, head_body, 0)

    row = lax.broadcasted_iota(jnp.int32, (BLOCK, 2 * BLOCK), 0)
    col = lax.broadcasted_iota(jnp.int32, (BLOCK, 2 * BLOCK), 1)
    dist = row + BLOCK - col
    local = (dist >= 0) & (dist < WINDOW)
    first_block = (r % blocks_per_seq) == 0
    valid = local & (jnp.logical_not(first_block) | (col >= BLOCK))
    scale = HEAD_DIM ** -0.5

    for b in range(N_KV // 2):
        sl = slice(b * LANES, (b + 1) * LANES)
        k_split = _split_kv_pair(jnp.concatenate([kp_ref[:, sl], kc_ref[:, sl]], axis=0))
        v_split = _split_kv_pair(jnp.concatenate([vp_ref[:, sl], vc_ref[:, sl]], axis=0))
        for gl in range(2):
            g = 2 * b + gl
            kv_lo_hi = tuple((k_split[gl][par], v_split[gl][par]) for par in range(2))
            q = jnp.concatenate(
                [q_ref[:, (pairs * g + a) * LANES:(pairs * g + a + 1) * LANES] for a in range(pairs)],
                axis=0)
            out = None
            for par in range(2):
                k_ext, v_ext = kv_lo_hi[par]
                s_all = lax.dot_general(q, k_ext, (((1,), (1,)), ((), ())),
                                        preferred_element_type=F32)
                ps = []
                for a in range(pairs):
                    h = GROUP * g + 2 * a + par
                    rs = slice(a * BLOCK, (a + 1) * BLOCK)
                    s = s_all[rs] * scale + bias_ref[2 * g + par, rs, :]
                    s = jnp.where(valid, s, NEG)
                    sink = sink_ref[h]
                    m = jnp.maximum(jnp.max(s, axis=-1, keepdims=True), sink)
                    p = jnp.exp(s - m)
                    den = jnp.sum(p, axis=-1, keepdims=True) + jnp.exp(sink - m)
                    ps.append((p / den).astype(BF16))
                p_all = jnp.concatenate(ps, axis=0)
                part = jnp.dot(p_all, v_ext, preferred_element_type=F32)
                out = part if out is None else out + part
            for a in range(pairs):
                o_ref[:, (pairs * g + a) * LANES:(pairs * g + a + 1) * LANES] = (
                    out[a * BLOCK:(a + 1) * BLOCK].astype(o_ref.dtype))


def swa_attention(qkv, sinks, rel_bias, *, seq):
    n = qkv.shape[0]
    nblk = n // BLOCK
    kcol = Q_W // KV_W
    bucket = jnp.asarray(_t5_bucket_table())
    smem = pl.BlockSpec(memory_space=pltpu.SMEM)
    return pl.pallas_call(
        functools.partial(_attn_kernel, blocks_per_seq=seq // BLOCK),
        grid=(nblk,),
        in_specs=[smem, smem,
                  pl.BlockSpec((BLOCK, 2 * BLOCK), lambda r: (0, 0)),
                  pl.BlockSpec((BLOCK, Q_W), lambda r: (r, 0)),
                  pl.BlockSpec((BLOCK, KV_W), lambda r: (r, kcol)),
                  pl.BlockSpec((BLOCK, KV_W), lambda r: (jnp.maximum(r - 1, 0), kcol)),
                  pl.BlockSpec((BLOCK, KV_W), lambda r: (r, kcol + 1)),
                  pl.BlockSpec((BLOCK, KV_W), lambda r: (jnp.maximum(r - 1, 0), kcol + 1))],
        out_specs=pl.BlockSpec((BLOCK, Q_W), lambda r: (r, 0)),
        out_shape=jax.ShapeDtypeStruct((n, Q_W), BF16),
        scratch_shapes=[pltpu.VMEM((2 * N_KV, (GROUP // 2) * BLOCK, 2 * BLOCK), F32)],
        compiler_params=_cparams(1),
        name="swa_attention",
    )(rel_bias, sinks, bucket, qkv, qkv, qkv, qkv, qkv)


def _cmul(x, y):
    return x[0] * y[0] - x[1] * y[1], x[0] * y[1] + x[1] * y[0]


def _s5_prep_kernel(lr_ref, li_ref, ls_ref, br_ref, bi_ref, cr_ref, wb_ref, wc_ref, coef_ref):
    ns = STATES_PER_CTILE
    lr = lr_ref[0]
    li = li_ref[0]
    delta = jnp.exp(ls_ref[0])
    mag = jnp.exp(lr * delta)
    ar = mag * jnp.cos(li * delta)
    ai = mag * jnp.sin(li * delta)
    nr = ar - 1.0
    ni = ai
    den = lr * lr + li * li
    fr = (nr * lr + ni * li) / den
    fi = (ni * lr - nr * li) / den
    is_re = lax.broadcasted_iota(jnp.int32, (1, 2 * ns), 1) < ns
    c_br = jnp.where(is_re, fr, fi)
    c_bi = jnp.where(is_re, -fi, fr)
    wb_ref[0] = (c_br * br_ref[0] + c_bi * bi_ref[0]).astype(BF16)
    is_re_row = lax.broadcasted_iota(jnp.int32, (2 * ns, LANES), 0) < ns
    cr = cr_ref[0]
    wc_ref[0] = jnp.where(is_re_row, cr, -cr).astype(BF16)
    a1 = (ar[:, :ns], ai[:, :ns])
    a2 = _cmul(a1, a1)
    a3 = _cmul(a2, a1)
    a4 = _cmul(a2, a2)
    pw = [a1, a2, a3, a4, _cmul(a4, a1), _cmul(a4, a2), _cmul(a4, a3), _cmul(a4, a4)]
    rowi = lax.broadcasted_iota(jnp.int32, (SUBLANES, ns), 0)
    zero = jnp.zeros((SUBLANES, ns), F32)
    for part in range(2):
        coef_ref[0, 0 + part] = jnp.where(rowi >= 1, a1[part], zero)
        coef_ref[0, 2 + part] = jnp.where(rowi >= 2, a2[part], zero)
        coef_ref[0, 4 + part] = jnp.where(rowi >= 4, a4[part], zero)
        p = zero
        for rr in range(SUBLANES):
            p = jnp.where(rowi == rr, pw[rr][part], p)
        coef_ref[0, 6 + part] = p


def s5_prepare(lam_re, lam_im, log_step, b_re, b_im, c_re, c_im):
    g, p = lam_re.shape
    gc = b_re.shape[-1]
    gpt = GROUPS_PER_CTILE
    ct = g // gpt
    ns = STATES_PER_CTILE
    same = jnp.arange(gpt)[:, None] == jnp.arange(gpt)[None, :]

    def cols(v):
        v = v.reshape(ct, 1, ns)
        return jnp.concatenate([v, v], axis=-1)

    def b_arr(b):
        b5 = jnp.transpose(b.reshape(ct, gpt, p, gc), (0, 3, 1, 2))[:, None]
        m = jnp.where(same[None, :, None, :, None], b5, 0.0)
        m = m.reshape(ct, gpt * gc, ns)
        return jnp.concatenate([m, m], axis=-1)

    def c_arr(c):
        c5 = jnp.transpose(c.reshape(ct, gpt, gc, p), (0, 1, 3, 2))[:, :, :, None]
        m = jnp.where(same[None, :, None, :, None], c5, 0.0)
        return m.reshape(ct, ns, gpt * gc)

    lr = cols(lam_re)
    li = cols(lam_im)
    ls = cols(jnp.broadcast_to(log_step[:, None], (g, p)))
    br = b_arr(b_re)
    bi = b_arr(b_im)
    cr = jnp.concatenate([c_arr(c_re), c_arr(c_im)], axis=1)
    vec = pl.BlockSpec((1, 1, 2 * ns), lambda c: (c, 0, 0))
    return pl.pallas_call(
        _s5_prep_kernel,
        grid=(ct,),
        in_specs=[vec, vec, vec,
                  pl.BlockSpec((1, LANES, 2 * ns), lambda c: (c, 0, 0)),
                  pl.BlockSpec((1, LANES, 2 * ns), lambda c: (c, 0, 0)),
                  pl.BlockSpec((1, 2 * ns, LANES), lambda c: (c, 0, 0))],
        out_specs=[pl.BlockSpec((1, LANES, 2 * ns), lambda c: (c, 0, 0)),
                   pl.BlockSpec((1, 2 * ns, LANES), lambda c: (c, 0, 0)),
                   pl.BlockSpec((1, 8, SUBLANES, ns), lambda c: (c, 0, 0, 0))],
        out_shape=[jax.ShapeDtypeStruct((ct, LANES, 2 * ns), BF16),
                   jax.ShapeDtypeStruct((ct, 2 * ns, LANES), BF16),
                   jax.ShapeDtypeStruct((ct, 8, SUBLANES, ns), F32)],
        compiler_params=_cparams(1),
        name="s5_prepare",
    )(lr, li, ls, br, bi, cr)


def _s5_kernel(x_ref, rs_ref, g_ref, d_ref, wb_ref, wc_ref, coef_ref, o_ref, st_ref, carry_ref,
               *, tm, tiles_per_seq):
    ns = STATES_PER_CTILE
    lane_tiles = ns // LANES
    i = pl.program_id(1)

    @pl.when(i % tiles_per_seq == 0)
    def _():
        carry_ref[...] = jnp.zeros(carry_ref.shape, F32)

    h = (x_ref[...] * rs_ref[...]) * g_ref[...]
    st_ref[...] = jnp.dot(h.astype(BF16), wb_ref[0], preferred_element_type=F32)

    def tile_body(k, carry):
        r0 = pl.multiple_of(k * SUBLANES, SUBLANES)
        new_carry = []
        for lt in range(lane_tiles):
            cs = slice(lt * LANES, (lt + 1) * LANES)
            ci = slice(ns + lt * LANES, ns + (lt + 1) * LANES)
            yr = st_ref[pl.ds(r0, SUBLANES), cs]
            yi = st_ref[pl.ds(r0, SUBLANES), ci]
            for step, shift in enumerate((1, 2, 4)):
                cr = coef_ref[0, 2 * step, :, cs]
                cim = coef_ref[0, 2 * step + 1, :, cs]
                sr = pltpu.roll(yr, shift, 0)
                si = pltpu.roll(yi, shift, 0)
                yr, yi = yr + (cr * sr - cim * si), yi + (cr * si + cim * sr)
            pr = coef_ref[0, 6, :, cs]
            pim = coef_ref[0, 7, :, cs]
            c_r, c_i = carry[2 * lt], carry[2 * lt + 1]
            xr = yr + (pr * c_r - pim * c_i)
            xi = yi + (pr * c_i + pim * c_r)
            st_ref[pl.ds(r0, SUBLANES), cs] = xr
            st_ref[pl.ds(r0, SUBLANES), ci] = xi
            new_carry.append(jnp.broadcast_to(xr[SUBLANES - 1:SUBLANES, :], (SUBLANES, LANES)))
            new_carry.append(jnp.broadcast_to(xi[SUBLANES - 1:SUBLANES, :], (SUBLANES, LANES)))
        return tuple(new_carry)

    carry0 = tuple(carry_ref[j] for j in range(2 * lane_tiles))
    carry = lax.fori_loop(0, tm // SUBLANES, tile_body, carry0)
    for j in range(2 * lane_tiles):
        carry_ref[j] = carry[j]

    y = jnp.dot(st_ref[...].astype(BF16), wc_ref[0], preferred_element_type=F32)
    y = y + d_ref[...] * h
    o_ref[...] = jax.nn.gelu(y).astype(o_ref.dtype)


def s5_mix(x, rowscale, norm_g, d_skip, wb, wc, coef, *, seq, tm):
    n, d = x.shape
    ct = d // LANES
    ns = STATES_PER_CTILE
    return pl.pallas_call(
        functools.partial(_s5_kernel, tm=tm, tiles_per_seq=seq // tm),
        grid=(ct, n // tm),
        in_specs=[pl.BlockSpec((tm, LANES), lambda c, i: (i, c)),
                  pl.BlockSpec((tm, 1), lambda c, i: (i, 0)),
                  pl.BlockSpec((1, LANES), lambda c, i: (0, c)),
                  pl.BlockSpec((1, LANES), lambda c, i: (0, c)),
                  pl.BlockSpec((1, LANES, 2 * ns), lambda c, i: (c, 0, 0)),
                  pl.BlockSpec((1, 2 * ns, LANES), lambda c, i: (c, 0, 0)),
                  pl.BlockSpec((1, 8, SUBLANES, ns), lambda c, i: (c, 0, 0, 0))],
        out_specs=pl.BlockSpec((tm, LANES), lambda c, i: (i, c)),
        out_shape=jax.ShapeDtypeStruct((n, d), BF16),
        scratch_shapes=[pltpu.VMEM((tm, 2 * ns), F32),
                        pltpu.VMEM((2 * (ns // LANES), SUBLANES, LANES), F32)],
        compiler_params=_cparams(2),
        name="s5_mix",
    )(x, rowscale, norm_g.reshape(1, d), d_skip.reshape(1, d), wb, wc, coef)


def conv_gated_mlp(x, norm_g, w_up, conv_w, conv_b, w_down, *, seq):
    h = rmsnorm(x, norm_g, BF16)
    a = up_conv_gate(h, w_up, conv_w, conv_b, seq=seq, tm=1024, tn=256)
    half = w_down.shape[0] // 2
    x = matmul(a, w_down, resid=x, out_dtype=F32, tm=512, tn=512, k_block=0, k_size=half)
    return matmul(a, w_down, resid=x, out_dtype=F32, tm=512, tn=512, k_block=1, k_size=half)


def kernel(x, attn_norm, w_qkv, w_o, sinks, rel_bias, ssm_norm, lambda_re, lambda_im, log_step, b_re, b_im, c_re, c_im, d_skip, w_glu, ffn_norm, w_up, conv_w, conv_b, w_down, final_norm):
    batch, seq, d = x.shape
    x = x.reshape(batch * seq, d)

    h = rmsnorm(x, attn_norm[0], BF16)
    qkv = matmul(h, w_qkv[0], out_dtype=BF16, tm=1024, tn=512)
    o = swa_attention(qkv, sinks[0], rel_bias, seq=seq)
    x = matmul(o, w_o[0], resid=x, out_dtype=F32, tm=1024, tn=512)
    x = conv_gated_mlp(x, ffn_norm[0], w_up[0], conv_w[0], conv_b[0], w_down[0], seq=seq)

    wb, wc, coef = s5_prepare(lambda_re[0], lambda_im[0], log_step[0], b_re[0], b_im[0], c_re[0], c_im[0])
    y = s5_mix(x, rms_rowscale(x), ssm_norm[0], d_skip[0], wb, wc, coef, seq=seq, tm=1024)
    x = glu_matmul(y, w_glu[0], x, tm=1024, tn=256)
    x = conv_gated_mlp(x, ffn_norm[1], w_up[1], conv_w[1], conv_b[1], w_down[1], seq=seq)

    return rmsnorm(x, final_norm, F32).reshape(batch, seq, d)
```

```python
import functools
import math

import numpy as np
import jax
import jax.numpy as jnp
from jax import lax
from jax.experimental import pallas as pl
from jax.experimental.pallas import tpu as pltpu

F32 = jnp.float32
BF16 = jnp.bfloat16

D_MODEL = 4096
SEQ = 4096
N_HEADS = 64
N_KV = 8
GROUP = N_HEADS // N_KV
HEAD_DIM = 64
Q_W = N_HEADS * HEAD_DIM
KV_W = N_KV * HEAD_DIM
WINDOW = 128
BLOCK = 128
NUM_BUCKETS = 32
MAX_EXACT = NUM_BUCKETS // 2
MAX_DISTANCE = 128
SSM_GC = 16
SSM_P = 64
D_FF = 11008
CONV_W = 3
EPS = 1e-6
NEG = -1e30

LANES = 128
SUBLANES = 8
VMEM_LIMIT_BYTES = 56 * 1024 * 1024

GROUPS_PER_CTILE = LANES // SSM_GC
STATES_PER_CTILE = GROUPS_PER_CTILE * SSM_P


def _cparams(n_axes):
    return pltpu.CompilerParams(dimension_semantics=("arbitrary",) * n_axes,
                                vmem_limit_bytes=VMEM_LIMIT_BYTES)


def _rmsnorm_kernel(x_ref, g_ref, o_ref):
    x = x_ref[...]
    y = x * lax.rsqrt(jnp.mean(x * x, axis=-1, keepdims=True) + EPS)
    o_ref[...] = (y * g_ref[...]).astype(o_ref.dtype)


def rmsnorm(x, g, layer, out_dtype, tr=512):
    n, d = x.shape
    return pl.pallas_call(
        _rmsnorm_kernel,
        grid=(n // tr,),
        in_specs=[pl.BlockSpec((tr, d), lambda i: (i, 0)),
                  pl.BlockSpec((None, 1, d), lambda i: (layer, 0, 0))],
        out_specs=pl.BlockSpec((tr, d), lambda i: (i, 0)),
        out_shape=jax.ShapeDtypeStruct((n, d), out_dtype),
        compiler_params=_cparams(1),
        name="rmsnorm",
    )(x, g.reshape(g.shape[0], 1, d))


def _rowscale_kernel(x_ref, o_ref):
    x = x_ref[...]
    o_ref[...] = lax.rsqrt(jnp.mean(x * x, axis=-1, keepdims=True) + EPS)


def rms_rowscale(x, tr=512):
    n, d = x.shape
    return pl.pallas_call(
        _rowscale_kernel,
        grid=(n // tr,),
        in_specs=[pl.BlockSpec((tr, d), lambda i: (i, 0))],
        out_specs=pl.BlockSpec((tr, 1), lambda i: (i, 0)),
        out_shape=jax.ShapeDtypeStruct((n, 1), F32),
        compiler_params=_cparams(1),
        name="rms_rowscale",
    )(x)


def _mm_kernel(a_ref, w_ref, *rest, has_resid):
    if has_resid:
        r_ref, o_ref, wb_ref = rest
    else:
        o_ref, wb_ref = rest

    @pl.when(pl.program_id(1) == 0)
    def _():
        wb_ref[...] = w_ref[...].astype(BF16)

    acc = jnp.dot(a_ref[...], wb_ref[...], preferred_element_type=F32)
    if has_resid:
        acc = r_ref[...] + acc
    o_ref[...] = acc.astype(o_ref.dtype)


def matmul(a, w, layer, *, resid=None, out_dtype, tm, tn, k_block=0, k_size=None):
    m = a.shape[0]
    n = w.shape[2]
    k_size = a.shape[1] if k_size is None else k_size
    in_specs = [pl.BlockSpec((tm, k_size), lambda j, i: (i, k_block)),
                pl.BlockSpec((None, k_size, tn), lambda j, i: (layer, k_block, j))]
    args = [a, w]
    if resid is not None:
        in_specs.append(pl.BlockSpec((tm, tn), lambda j, i: (i, j)))
        args.append(resid)
    return pl.pallas_call(
        functools.partial(_mm_kernel, has_resid=resid is not None),
        grid=(n // tn, m // tm),
        in_specs=in_specs,
        out_specs=pl.BlockSpec((tm, tn), lambda j, i: (i, j)),
        out_shape=jax.ShapeDtypeStruct((m, n), out_dtype),
        scratch_shapes=[pltpu.VMEM((k_size, tn), BF16)],
        compiler_params=_cparams(2),
        name="matmul",
    )(*args)


def _glu_mm_kernel(a_ref, w1_ref, w2_ref, r_ref, o_ref, wb_ref, *, tn):
    @pl.when(pl.program_id(1) == 0)
    def _():
        wb_ref[:, :tn] = w1_ref[...].astype(BF16)
        wb_ref[:, tn:] = w2_ref[...].astype(BF16)

    z = jnp.dot(a_ref[...], wb_ref[...], preferred_element_type=F32)
    o_ref[...] = r_ref[...] + z[:, :tn] * jax.nn.sigmoid(z[:, tn:])


def glu_matmul(a, w, layer, resid, *, tm, tn):
    m, k = a.shape
    d = w.shape[2] // 2
    nt = d // tn
    return pl.pallas_call(
        functools.partial(_glu_mm_kernel, tn=tn),
        grid=(nt, m // tm),
        in_specs=[pl.BlockSpec((tm, k), lambda j, i: (i, 0)),
                  pl.BlockSpec((None, k, tn), lambda j, i: (layer, 0, j)),
                  pl.BlockSpec((None, k, tn), lambda j, i: (layer, 0, j + nt)),
                  pl.BlockSpec((tm, tn), lambda j, i: (i, j))],
        out_specs=pl.BlockSpec((tm, tn), lambda j, i: (i, j)),
        out_shape=jax.ShapeDtypeStruct((m, d), F32),
        scratch_shapes=[pltpu.VMEM((k, 2 * tn), BF16)],
        compiler_params=_cparams(2),
        name="glu_matmul",
    )(a, w, w, resid)


def _up_conv_gate_kernel(a_ref, wg_ref, wv_ref, cwg_ref, cwv_ref, cbg_ref, cbv_ref, o_ref,
                         wb_ref, ubuf_ref, *, tm, tn, tiles_per_seq):
    i = pl.program_id(1)

    @pl.when(i == 0)
    def _():
        wb_ref[:, :tn] = wg_ref[...].astype(BF16)
        wb_ref[:, tn:] = wv_ref[...].astype(BF16)

    @pl.when(i % tiles_per_seq == 0)
    def _():
        ubuf_ref[0:SUBLANES, :] = jnp.zeros((SUBLANES, 2 * tn), F32)

    u = jnp.dot(a_ref[...], wb_ref[...], preferred_element_type=F32)
    ubuf_ref[SUBLANES:SUBLANES + tm, :] = u
    u1 = ubuf_ref[SUBLANES - 1:SUBLANES - 1 + tm, :]
    u2 = ubuf_ref[SUBLANES - 2:SUBLANES - 2 + tm, :]
    cw = jnp.concatenate([cwg_ref[...], cwv_ref[...]], axis=-1)
    cb = jnp.concatenate([cbg_ref[...], cbv_ref[...]], axis=-1)
    acc = cw[0:1] * u2
    acc = acc + cw[1:2] * u1
    acc = acc + cw[2:3] * u
    c = cb + acc
    g = c[:, :tn]
    o_ref[...] = (g * jax.nn.sigmoid(g) * c[:, tn:]).astype(o_ref.dtype)
    ubuf_ref[0:SUBLANES, :] = ubuf_ref[tm:tm + SUBLANES, :]


def up_conv_gate(a, w_up, conv_w, conv_b, layer, *, seq, tm, tn):
    m, k = a.shape
    f = w_up.shape[2] // 2
    nt = f // tn
    conv_b3 = conv_b.reshape(conv_b.shape[0], 1, 2 * f)
    return pl.pallas_call(
        functools.partial(_up_conv_gate_kernel, tm=tm, tn=tn, tiles_per_seq=seq // tm),
        grid=(nt, m // tm),
        in_specs=[pl.BlockSpec((tm, k), lambda j, i: (i, 0)),
                  pl.BlockSpec((None, k, tn), lambda j, i: (layer, 0, j)),
                  pl.BlockSpec((None, k, tn), lambda j, i: (layer, 0, j + nt)),
                  pl.BlockSpec((None, CONV_W, tn), lambda j, i: (layer, 0, j)),
                  pl.BlockSpec((None, CONV_W, tn), lambda j, i: (layer, 0, j + nt)),
                  pl.BlockSpec((None, 1, tn), lambda j, i: (layer, 0, j)),
                  pl.BlockSpec((None, 1, tn), lambda j, i: (layer, 0, j + nt))],
        out_specs=pl.BlockSpec((tm, tn), lambda j, i: (i, j)),
        out_shape=jax.ShapeDtypeStruct((m, f), BF16),
        scratch_shapes=[pltpu.VMEM((k, 2 * tn), BF16),
                        pltpu.VMEM((tm + SUBLANES, 2 * tn), F32)],
        compiler_params=_cparams(2),
        name="up_conv_gate",
    )(a, w_up, w_up, conv_w, conv_w, conv_b3, conv_b3)


def _t5_bucket_table():
    qi = np.arange(BLOCK)[:, None] + BLOCK
    kj = np.arange(2 * BLOCK)[None, :]
    n = np.maximum(qi - kj, 0)
    is_small = n < MAX_EXACT
    large = MAX_EXACT + (np.log(np.maximum(n, 1) / MAX_EXACT) / np.log(MAX_DISTANCE / MAX_EXACT)
                         * (NUM_BUCKETS - MAX_EXACT)).astype(np.int32)
    large = np.minimum(large, NUM_BUCKETS - 1)
    return np.where(is_small, n, large).astype(np.int32)


def _split_kv_pair(x):
    u = pltpu.bitcast(x, jnp.uint32)
    ru = pltpu.roll(u, LANES // 2, 1)
    lo = lax.broadcasted_iota(jnp.int32, u.shape, 1) < LANES // 2
    zero = jnp.zeros(u.shape, jnp.uint32)
    as_bf16 = lambda v: pltpu.bitcast(v, BF16)
    even = (as_bf16(jnp.where(lo, u, zero)), as_bf16(jnp.where(lo, zero, ru)))
    odd = (as_bf16(jnp.where(lo, ru, zero)), as_bf16(jnp.where(lo, zero, u)))
    return even, odd


def _attn_kernel(relb_ref, sink_ref, bucket_ref, q_ref, kc_ref, kp_ref, vc_ref, vp_ref, o_ref,
                 bias_ref, *, blocks_per_seq):
    r = pl.program_id(0)
    pairs = GROUP // 2
    qrows = pairs * BLOCK

    @pl.when(r == 0)
    def _():
        bucket = bucket_ref[...]

        def head_body(h, carry):
            tile = jnp.zeros((BLOCK, 2 * BLOCK), F32)
            for b in range(NUM_BUCKETS):
                tile = jnp.where(bucket == b, relb_ref[b, h], tile)
            g = h // GROUP
            a = (h % GROUP) // 2
            par = h % 2
            bias_ref[2 * g + par, pl.ds(pl.multiple_of(a * BLOCK, BLOCK), BLOCK), :] = tile
            return carry

        lax.fori_loop(0, N_HEADS, head_body, 0)

    row = lax.broadcasted_iota(jnp.int32, (BLOCK, 2 * BLOCK), 0)
    col = lax.broadcasted_iota(jnp.int32, (BLOCK, 2 * BLOCK), 1)
    dist = row + BLOCK - col
    local = (dist >= 0) & (dist < WINDOW)
    first_block = (r % blocks_per_seq) == 0
    valid = local & (jnp.logical_not(first_block) | (col >= BLOCK))
    scale = HEAD_DIM ** -0.5

    for b in range(N_KV // 2):
        sl = slice(b * LANES, (b + 1) * LANES)
        k_split = _split_kv_pair(jnp.concatenate([kp_ref[:, sl], kc_ref[:, sl]], axis=0))
        v_split = _split_kv_pair(jnp.concatenate([vp_ref[:, sl], vc_ref[:, sl]], axis=0))
        for gl in range(2):
            g = 2 * b + gl
            kv_lo_hi = tuple((k_split[gl][par], v_split[gl][par]) for par in range(2))
            q = jnp.concatenate(
                [q_ref[:, (pairs * g + a) * LANES:(pairs * g + a + 1) * LANES] for a in range(pairs)],
                axis=0)
            out = None
            for par in range(2):
                k_ext, v_ext = kv_lo_hi[par]
                s_all = lax.dot_general(q, k_ext, (((1,), (1,)), ((), ())),
                                        preferred_element_type=F32)
                ps = []
                for a in range(pairs):
                    h = GROUP * g + 2 * a + par
                    rs = slice(a * BLOCK, (a + 1) * BLOCK)
                    s = s_all[rs] * scale + bias_ref[2 * g + par, rs, :]
                    s = jnp.where(valid, s, NEG)
                    sink = sink_ref[h]
                    m = jnp.maximum(jnp.max(s, axis=-1, keepdims=True), sink)
                    p = jnp.exp(s - m)
                    den = jnp.sum(p, axis=-1, keepdims=True) + jnp.exp(sink - m)
                    ps.append((p / den).astype(BF16))
                p_all = jnp.concatenate(ps, axis=0)
                part = jnp.dot(p_all, v_ext, preferred_element_type=F32)
                out = part if out is None else out + part
            for a in range(pairs):
                o_ref[:, (pairs * g + a) * LANES:(pairs * g + a + 1) * LANES] = (
                    out[a * BLOCK:(a + 1) * BLOCK].astype(o_ref.dtype))


def swa_attention(qkv, sinks, rel_bias, *, seq):
    n = qkv.shape[0]
    nblk = n // BLOCK
    kcol = Q_W // KV_W
    bucket = jnp.asarray(_t5_bucket_table())
    smem = pl.BlockSpec(memory_space=pltpu.SMEM)
    return pl.pallas_call(
        functools.partial(_attn_kernel, blocks_per_seq=seq // BLOCK),
        grid=(nblk,),
        in_specs=[smem, smem,
                  pl.BlockSpec((BLOCK, 2 * BLOCK), lambda r: (0, 0)),
                  pl.BlockSpec((BLOCK, Q_W), lambda r: (r, 0)),
                  pl.BlockSpec((BLOCK, KV_W), lambda r: (r, kcol)),
                  pl.BlockSpec((BLOCK, KV_W), lambda r: (jnp.maximum(r - 1, 0), kcol)),
                  pl.BlockSpec((BLOCK, KV_W), lambda r: (r, kcol + 1)),
                  pl.BlockSpec((BLOCK, KV_W), lambda r: (jnp.maximum(r - 1, 0), kcol + 1))],
        out_specs=pl.BlockSpec((BLOCK, Q_W), lambda r: (r, 0)),
        out_shape=jax.ShapeDtypeStruct((n, Q_W), BF16),
        scratch_shapes=[pltpu.VMEM((2 * N_KV, (GROUP // 2) * BLOCK, 2 * BLOCK), F32)],
        compiler_params=_cparams(1),
        name="swa_attention",
    )(rel_bias, sinks, bucket, qkv, qkv, qkv, qkv, qkv)


def _cmul(x, y):
    return x[0] * y[0] - x[1] * y[1], x[0] * y[1] + x[1] * y[0]


def _s5_prep_kernel(lr_ref, li_ref, ls_ref, br_ref, bi_ref, cr_ref, wb_ref, wc_ref, coef_ref):
    ns = STATES_PER_CTILE
    lr = lr_ref[0]
    li = li_ref[0]
    delta = jnp.exp(ls_ref[0])
    mag = jnp.exp(lr * delta)
    ar = mag * jnp.cos(li * delta)
    ai = mag * jnp.sin(li * delta)
    nr = ar - 1.0
    ni = ai
    den = lr * lr + li * li
    fr = (nr * lr + ni * li) / den
    fi = (ni * lr - nr * li) / den
    is_re = lax.broadcasted_iota(jnp.int32, (1, 2 * ns), 1) < ns
    c_br = jnp.where(is_re, fr, fi)
    c_bi = jnp.where(is_re, -fi, fr)
    wb_ref[0] = (c_br * br_ref[0] + c_bi * bi_ref[0]).astype(BF16)
    is_re_row = lax.broadcasted_iota(jnp.int32, (2 * ns, LANES), 0) < ns
    cr = cr_ref[0]
    wc_ref[0] = jnp.where(is_re_row, cr, -cr).astype(BF16)
    a1 = (ar[:, :ns], ai[:, :ns])
    a2 = _cmul(a1, a1)
    a3 = _cmul(a2, a1)
    a4 = _cmul(a2, a2)
    pw = [a1, a2, a3, a4, _cmul(a4, a1), _cmul(a4, a2), _cmul(a4, a3), _cmul(a4, a4)]
    rowi = lax.broadcasted_iota(jnp.int32, (SUBLANES, ns), 0)
    zero = jnp.zeros((SUBLANES, ns), F32)
    for part in range(2):
        coef_ref[0, 0 + part] = jnp.where(rowi >= 1, a1[part], zero)
        coef_ref[0, 2 + part] = jnp.where(rowi >= 2, a2[part], zero)
        coef_ref[0, 4 + part] = jnp.where(rowi >= 4, a4[part], zero)
        p = zero
        for rr in range(SUBLANES):
            p = jnp.where(rowi == rr, pw[rr][part], p)
        coef_ref[0, 6 + part] = p


def s5_prepare(lam_re, lam_im, log_step, b_re, b_im, c_re, c_im):
    g, p = lam_re.shape
    gc = b_re.shape[-1]
    gpt = GROUPS_PER_CTILE
    ct = g // gpt
    ns = STATES_PER_CTILE
    same = jnp.arange(gpt)[:, None] == jnp.arange(gpt)[None, :]

    def cols(v):
        v = v.reshape(ct, 1, ns)
        return jnp.concatenate([v, v], axis=-1)

    def b_arr(b):
        b5 = jnp.transpose(b.reshape(ct, gpt, p, gc), (0, 3, 1, 2))[:, None]
        m = jnp.where(same[None, :, None, :, None], b5, 0.0)
        m = m.reshape(ct, gpt * gc, ns)
        return jnp.concatenate([m, m], axis=-1)

    def c_arr(c):
        c5 = jnp.transpose(c.reshape(ct, gpt, gc, p), (0, 1, 3, 2))[:, :, :, None]
        m = jnp.where(same[None, :, None, :, None], c5, 0.0)
        return m.reshape(ct, ns, gpt * gc)

    lr = cols(lam_re)
    li = cols(lam_im)
    ls = cols(jnp.broadcast_to(log_step[:, None], (g, p)))
    br = b_arr(b_re)
    bi = b_arr(b_im)
    cr = jnp.concatenate([c_arr(c_re), c_arr(c_im)], axis=1)
    vec = pl.BlockSpec((1, 1, 2 * ns), lambda c: (c, 0, 0))
    return pl.pallas_call(
        _s5_prep_kernel,
        grid=(ct,),
        in_specs=[vec, vec, vec,
                  pl.BlockSpec((1, LANES, 2 * ns), lambda c: (c, 0, 0)),
                  pl.BlockSpec((1, LANES, 2 * ns), lambda c: (c, 0, 0)),
                  pl.BlockSpec((1, 2 * ns, LANES), lambda c: (c, 0, 0))],
        out_specs=[pl.BlockSpec((1, LANES, 2 * ns), lambda c: (c, 0, 0)),
                   pl.BlockSpec((1, 2 * ns, LANES), lambda c: (c, 0, 0)),
                   pl.BlockSpec((1, 8, SUBLANES, ns), lambda c: (c, 0, 0, 0))],
        out_shape=[jax.ShapeDtypeStruct((ct, LANES, 2 * ns), BF16),
                   jax.ShapeDtypeStruct((ct, 2 * ns, LANES), BF16),
                   jax.ShapeDtypeStruct((ct, 8, SUBLANES, ns), F32)],
        compiler_params=_cparams(1),
        name="s5_prepare",
    )(lr, li, ls, br, bi, cr)


def _s5_kernel(x_ref, rs_ref, g_ref, d_ref, wb_ref, wc_ref, coef_ref, o_ref, st_ref, carry_ref,
               *, tm, tiles_per_seq):
    ns = STATES_PER_CTILE
    lane_tiles = ns // LANES
    i = pl.program_id(1)

    @pl.when(i % tiles_per_seq == 0)
    def _():
        carry_ref[...] = jnp.zeros(carry_ref.shape, F32)

    h = (x_ref[...] * rs_ref[...]) * g_ref[...]
    st_ref[...] = jnp.dot(h.astype(BF16), wb_ref[0], preferred_element_type=F32)

    def tile_body(k, carry):
        r0 = pl.multiple_of(k * SUBLANES, SUBLANES)
        new_carry = []
        for lt in range(lane_tiles):
            cs = slice(lt * LANES, (lt + 1) * LANES)
            ci = slice(ns + lt * LANES, ns + (lt + 1) * LANES)
            yr = st_ref[pl.ds(r0, SUBLANES), cs]
            yi = st_ref[pl.ds(r0, SUBLANES), ci]
            for step, shift in enumerate((1, 2, 4)):
                cr = coef_ref[0, 2 * step, :, cs]
                cim = coef_ref[0, 2 * step + 1, :, cs]
                sr = pltpu.roll(yr, shift, 0)
                si = pltpu.roll(yi, shift, 0)
                yr, yi = yr + (cr * sr - cim * si), yi + (cr * si + cim * sr)
            pr = coef_ref[0, 6, :, cs]
            pim = coef_ref[0, 7, :, cs]
            c_r, c_i = carry[2 * lt], carry[2 * lt + 1]
            xr = yr + (pr * c_r - pim * c_i)
            xi = yi + (pr * c_i + pim * c_r)
            st_ref[pl.ds(r0, SUBLANES), cs] = xr
            st_ref[pl.ds(r0, SUBLANES), ci] = xi
            new_carry.append(jnp.broadcast_to(xr[SUBLANES - 1:SUBLANES, :], (SUBLANES, LANES)))
            new_carry.append(jnp.broadcast_to(xi[SUBLANES - 1:SUBLANES, :], (SUBLANES, LANES)))
        return tuple(new_carry)

    carry0 = tuple(carry_ref[j] for j in range(2 * lane_tiles))
    carry = lax.fori_loop(0, tm // SUBLANES, tile_body, carry0)
    for j in range(2 * lane_tiles):
        carry_ref[j] = carry[j]

    y = jnp.dot(st_ref[...].astype(BF16), wc_ref[0], preferred_element_type=F32)
    y = y + d_ref[...] * h
    o_ref[...] = jax.nn.gelu(y).astype(o_ref.dtype)


def s5_mix(x, rowscale, norm_g, d_skip, wb, wc, coef, *, seq, tm):
    n, d = x.shape
    ct = d // LANES
    ns = STATES_PER_CTILE
    return pl.pallas_call(
        functools.partial(_s5_kernel, tm=tm, tiles_per_seq=seq // tm),
        grid=(ct, n // tm),
        in_specs=[pl.BlockSpec((tm, LANES), lambda c, i: (i, c)),
                  pl.BlockSpec((tm, 1), lambda c, i: (i, 0)),
                  pl.BlockSpec((1, LANES), lambda c, i: (0, c)),
                  pl.BlockSpec((1, LANES), lambda c, i: (0, c)),
                  pl.BlockSpec((1, LANES, 2 * ns), lambda c, i: (c, 0, 0)),
                  pl.BlockSpec((1, 2 * ns, LANES), lambda c, i: (c, 0, 0)),
                  pl.BlockSpec((1, 8, SUBLANES, ns), lambda c, i: (c, 0, 0, 0))],
        out_specs=pl.BlockSpec((tm, LANES), lambda c, i: (i, c)),
        out_shape=jax.ShapeDtypeStruct((n, d), BF16),
        scratch_shapes=[pltpu.VMEM((tm, 2 * ns), F32),
                        pltpu.VMEM((2 * (ns // LANES), SUBLANES, LANES), F32)],
        compiler_params=_cparams(2),
        name="s5_mix",
    )(x, rowscale, norm_g.reshape(1, d), d_skip.reshape(1, d), wb, wc, coef)


def conv_gated_mlp(x, norm_g, w_up, conv_w, conv_b, w_down, layer, *, seq):
    h = rmsnorm(x, norm_g, layer, BF16)
    a = up_conv_gate(h, w_up, conv_w, conv_b, layer, seq=seq, tm=1024, tn=256)
    half = w_down.shape[1] // 2
    x = matmul(a, w_down, layer, resid=x, out_dtype=F32, tm=512, tn=512, k_block=0, k_size=half)
    return matmul(a, w_down, layer, resid=x, out_dtype=F32, tm=512, tn=512, k_block=1, k_size=half)


def kernel(x, attn_norm, w_qkv, w_o, sinks, rel_bias, ssm_norm, lambda_re, lambda_im, log_step, b_re, b_im, c_re, c_im, d_skip, w_glu, ffn_norm, w_up, conv_w, conv_b, w_down, final_norm):
    batch, seq, d = x.shape
    x = x.reshape(batch * seq, d)

    h = rmsnorm(x, attn_norm, 0, BF16)
    qkv = matmul(h, w_qkv, 0, out_dtype=BF16, tm=1024, tn=512)
    o = swa_attention(qkv, sinks[0], rel_bias, seq=seq)
    x = matmul(o, w_o, 0, resid=x, out_dtype=F32, tm=1024, tn=512)
    x = conv_gated_mlp(x, ffn_norm, w_up, conv_w, conv_b, w_down, 0, seq=seq)

    wb, wc, coef = s5_prepare(lambda_re[0], lambda_im[0], log_step[0], b_re[0], b_im[0], c_re[0], c_im[0])
    y = s5_mix(x, rms_rowscale(x), ssm_norm[0], d_skip[0], wb, wc, coef, seq=seq, tm=1024)
    x = glu_matmul(y, w_glu, 0, x, tm=1024, tn=256)
    x = conv_gated_mlp(x, ffn_norm, w_up, conv_w, conv_b, w_down, 1, seq=seq)

    return rmsnorm(x, final_norm.reshape(1, d), 0, F32).reshape(batch, seq, d)
```

```python
import functools
import math

import numpy as np
import jax
import jax.numpy as jnp
from jax import lax
from jax.experimental import pallas as pl
from jax.experimental.pallas import tpu as pltpu

F32 = jnp.float32
BF16 = jnp.bfloat16

D_MODEL = 4096
SEQ = 4096
N_HEADS = 64
N_KV = 8
GROUP = N_HEADS // N_KV
HEAD_DIM = 64
Q_W = N_HEADS * HEAD_DIM
KV_W = N_KV * HEAD_DIM
WINDOW = 128
BLOCK = 128
NUM_BUCKETS = 32
MAX_EXACT = NUM_BUCKETS // 2
MAX_DISTANCE = 128
SSM_GC = 16
SSM_P = 64
D_FF = 11008
CONV_W = 3
EPS = 1e-6
NEG = -1e30

LANES = 128
SUBLANES = 8
VMEM_LIMIT_BYTES = 56 * 1024 * 1024

GROUPS_PER_CTILE = LANES // SSM_GC
STATES_PER_CTILE = GROUPS_PER_CTILE * SSM_P


def _cparams(n_axes):
    return pltpu.CompilerParams(dimension_semantics=("arbitrary",) * n_axes,
                                vmem_limit_bytes=VMEM_LIMIT_BYTES)


def _rmsnorm_kernel(x_ref, g_ref, o_ref):
    x = x_ref[...]
    y = x * lax.rsqrt(jnp.mean(x * x, axis=-1, keepdims=True) + EPS)
    o_ref[...] = (y * g_ref[...]).astype(o_ref.dtype)


def rmsnorm(x, g, layer, out_dtype, tr=512):
    n, d = x.shape
    return pl.pallas_call(
        _rmsnorm_kernel,
        grid=(n // tr,),
        in_specs=[pl.BlockSpec((tr, d), lambda i: (i, 0)),
                  pl.BlockSpec((None, 1, d), lambda i: (layer, 0, 0))],
        out_specs=pl.BlockSpec((tr, d), lambda i: (i, 0)),
        out_shape=jax.ShapeDtypeStruct((n, d), out_dtype),
        compiler_params=_cparams(1),
        name="rmsnorm",
    )(x, g.reshape(g.shape[0], 1, d))


def _rowscale_kernel(x_ref, o_ref):
    x = x_ref[...]
    o_ref[...] = lax.rsqrt(jnp.mean(x * x, axis=-1, keepdims=True) + EPS)


def rms_rowscale(x, tr=512):
    n, d = x.shape
    return pl.pallas_call(
        _rowscale_kernel,
        grid=(n // tr,),
        in_specs=[pl.BlockSpec((tr, d), lambda i: (i, 0))],
        out_specs=pl.BlockSpec((tr, 1), lambda i: (i, 0)),
        out_shape=jax.ShapeDtypeStruct((n, 1), F32),
        compiler_params=_cparams(1),
        name="rms_rowscale",
    )(x)


def _mm_kernel(a_ref, w_ref, *rest, has_resid):
    if has_resid:
        r_ref, o_ref, wb_ref = rest
    else:
        o_ref, wb_ref = rest

    @pl.when(pl.program_id(1) == 0)
    def _():
        wb_ref[...] = w_ref[...].astype(BF16)

    acc = jnp.dot(a_ref[...], wb_ref[...], preferred_element_type=F32)
    if has_resid:
        acc = r_ref[...] + acc
    o_ref[...] = acc.astype(o_ref.dtype)


def matmul(a, w, layer, *, resid=None, out_dtype, tm, tn, k_block=0, k_size=None):
    m = a.shape[0]
    n = w.shape[2]
    k_size = a.shape[1] if k_size is None else k_size
    in_specs = [pl.BlockSpec((tm, k_size), lambda j, i: (i, k_block)),
                pl.BlockSpec((None, k_size, tn), lambda j, i: (layer, k_block, j))]
    args = [a, w]
    if resid is not None:
        in_specs.append(pl.BlockSpec((tm, tn), lambda j, i: (i, j)))
        args.append(resid)
    return pl.pallas_call(
        functools.partial(_mm_kernel, has_resid=resid is not None),
        grid=(n // tn, m // tm),
        in_specs=in_specs,
        out_specs=pl.BlockSpec((tm, tn), lambda j, i: (i, j)),
        out_shape=jax.ShapeDtypeStruct((m, n), out_dtype),
        scratch_shapes=[pltpu.VMEM((k_size, tn), BF16)],
        compiler_params=_cparams(2),
        name="matmul",
    )(*args)


def _glu_mm_kernel(a_ref, w1_ref, w2_ref, r_ref, o_ref, wb_ref, *, tn):
    @pl.when(pl.program_id(1) == 0)
    def _():
        wb_ref[:, :tn] = w1_ref[...].astype(BF16)
        wb_ref[:, tn:] = w2_ref[...].astype(BF16)

    z = jnp.dot(a_ref[...], wb_ref[...], preferred_element_type=F32)
    o_ref[...] = r_ref[...] + z[:, :tn] * jax.nn.sigmoid(z[:, tn:])


def glu_matmul(a, w, layer, resid, *, tm, tn):
    m, k = a.shape
    d = w.shape[2] // 2
    nt = d // tn
    return pl.pallas_call(
        functools.partial(_glu_mm_kernel, tn=tn),
        grid=(nt, m // tm),
        in_specs=[pl.BlockSpec((tm, k), lambda j, i: (i, 0)),
                  pl.BlockSpec((None, k, tn), lambda j, i: (layer, 0, j)),
                  pl.BlockSpec((None, k, tn), lambda j, i: (layer, 0, j + nt)),
                  pl.BlockSpec((tm, tn), lambda j, i: (i, j))],
        out_specs=pl.BlockSpec((tm, tn), lambda j, i: (i, j)),
        out_shape=jax.ShapeDtypeStruct((m, d), F32),
        scratch_shapes=[pltpu.VMEM((k, 2 * tn), BF16)],
        compiler_params=_cparams(2),
        name="glu_matmul",
    )(a, w, w, resid)


def _up_conv_gate_kernel(a_ref, wg_ref, wv_ref, cwg_ref, cwv_ref, cbg_ref, cbv_ref, o_ref,
                         wb_ref, ubuf_ref, *, tm, tn, tiles_per_seq):
    i = pl.program_id(1)

    @pl.when(i == 0)
    def _():
        wb_ref[:, :tn] = wg_ref[...].astype(BF16)
        wb_ref[:, tn:] = wv_ref[...].astype(BF16)

    @pl.when(i % tiles_per_seq == 0)
    def _():
        ubuf_ref[0:SUBLANES, :] = jnp.zeros((SUBLANES, 2 * tn), F32)

    u = jnp.dot(a_ref[...], wb_ref[...], preferred_element_type=F32)
    ubuf_ref[SUBLANES:SUBLANES + tm, :] = u
    u1 = ubuf_ref[SUBLANES - 1:SUBLANES - 1 + tm, :]
    u2 = ubuf_ref[SUBLANES - 2:SUBLANES - 2 + tm, :]
    cw = jnp.concatenate([cwg_ref[...], cwv_ref[...]], axis=-1)
    cb = jnp.concatenate([cbg_ref[...], cbv_ref[...]], axis=-1)
    acc = cw[0:1] * u2
    acc = acc + cw[1:2] * u1
    acc = acc + cw[2:3] * u
    c = cb + acc
    g = c[:, :tn]
    o_ref[...] = (g * jax.nn.sigmoid(g) * c[:, tn:]).astype(o_ref.dtype)
    ubuf_ref[0:SUBLANES, :] = ubuf_ref[tm:tm + SUBLANES, :]


def up_conv_gate(a, w_up, conv_w, conv_b, layer, *, seq, tm, tn):
    m, k = a.shape
    f = w_up.shape[2] // 2
    nt = f // tn
    conv_b3 = conv_b.reshape(conv_b.shape[0], 1, 2 * f)
    return pl.pallas_call(
        functools.partial(_up_conv_gate_kernel, tm=tm, tn=tn, tiles_per_seq=seq // tm),
        grid=(nt, m // tm),
        in_specs=[pl.BlockSpec((tm, k), lambda j, i: (i, 0)),
                  pl.BlockSpec((None, k, tn), lambda j, i: (layer, 0, j)),
                  pl.BlockSpec((None, k, tn), lambda j, i: (layer, 0, j + nt)),
                  pl.BlockSpec((None, CONV_W, tn), lambda j, i: (layer, 0, j)),
                  pl.BlockSpec((None, CONV_W, tn), lambda j, i: (layer, 0, j + nt)),
                  pl.BlockSpec((None, 1, tn), lambda j, i: (layer, 0, j)),
                  pl.BlockSpec((None, 1, tn), lambda j, i: (layer, 0, j + nt))],
        out_specs=pl.BlockSpec((tm, tn), lambda j, i: (i, j)),
        out_shape=jax.ShapeDtypeStruct((m, f), BF16),
        scratch_shapes=[pltpu.VMEM((k, 2 * tn), BF16),
                        pltpu.VMEM((tm + SUBLANES, 2 * tn), F32)],
        compiler_params=_cparams(2),
        name="up_conv_gate",
    )(a, w_up, w_up, conv_w, conv_w, conv_b3, conv_b3)


def _t5_bucket_table():
    qi = np.arange(BLOCK)[:, None] + BLOCK
    kj = np.arange(2 * BLOCK)[None, :]
    n = np.maximum(qi - kj, 0)
    is_small = n < MAX_EXACT
    large = MAX_EXACT + (np.log(np.maximum(n, 1) / MAX_EXACT) / np.log(MAX_DISTANCE / MAX_EXACT)
                         * (NUM_BUCKETS - MAX_EXACT)).astype(np.int32)
    large = np.minimum(large, NUM_BUCKETS - 1)
    return np.where(is_small, n, large).astype(np.int32)


def _split_kv_pair(x):
    u = pltpu.bitcast(x, jnp.uint32)
    ru = pltpu.roll(u, LANES // 2, 1)
    lo = lax.broadcasted_iota(jnp.int32, u.shape, 1) < LANES // 2
    zero = jnp.zeros(u.shape, jnp.uint32)
    as_bf16 = lambda v: pltpu.bitcast(v, BF16)
    even = (as_bf16(jnp.where(lo, u, zero)), as_bf16(jnp.where(lo, zero, ru)))
    odd = (as_bf16(jnp.where(lo, ru, zero)), as_bf16(jnp.where(lo, zero, u)))
    return even, odd


def _attn_kernel(relb_ref, sink_ref, bucket_ref, q_ref, kc_ref, kp_ref, vc_ref, vp_ref, o_ref,
                 bias_ref, *, blocks_per_seq):
    r = pl.program_id(0)
    pairs = GROUP // 2
    qrows = pairs * BLOCK

    @pl.when(r == 0)
    def _():
        bucket = bucket_ref[...]

        def head_body(h, carry):
            tile = jnp.zeros((BLOCK, 2 * BLOCK), F32)
            for b in range(NUM_BUCKETS):
                tile = jnp.where(bucket == b, relb_ref[b, h], tile)
            g = h // GROUP
            a = (h % GROUP) // 2
            par = h % 2
            bias_ref[2 * g + par, pl.ds(pl.multiple_of(a * BLOCK, BLOCK), BLOCK), :] = tile
            return carry

        lax.fori_loop(0, N_HEADS, head_body, 0)

    row = lax.broadcasted_iota(jnp.int32, (BLOCK, 2 * BLOCK), 0)
    col = lax.broadcasted_iota(jnp.int32, (BLOCK, 2 * BLOCK), 1)
    dist = row + BLOCK - col
    local = (dist >= 0) & (dist < WINDOW)
    first_block = (r % blocks_per_seq) == 0
    valid = local & (jnp.logical_not(first_block) | (col >= BLOCK))
    scale = HEAD_DIM ** -0.5

    for b in range(N_KV // 2):
        sl = slice(b * LANES, (b + 1) * LANES)
        k_split = _split_kv_pair(jnp.concatenate([kp_ref[:, sl], kc_ref[:, sl]], axis=0))
        v_split = _split_kv_pair(jnp.concatenate([vp_ref[:, sl], vc_ref[:, sl]], axis=0))
        for gl in range(2):
            g = 2 * b + gl
            kv_lo_hi = tuple((k_split[gl][par], v_split[gl][par]) for par in range(2))
            q = jnp.concatenate(
                [q_ref[:, (pairs * g + a) * LANES:(pairs * g + a + 1) * LANES] for a in range(pairs)],
                axis=0)
            out = None
            for par in range(2):
                k_ext, v_ext = kv_lo_hi[par]
                s_all = lax.dot_general(q, k_ext, (((1,), (1,)), ((), ())),
                                        preferred_element_type=F32)
                ps = []
                for a in range(pairs):
                    h = GROUP * g + 2 * a + par
                    rs = slice(a * BLOCK, (a + 1) * BLOCK)
                    s = s_all[rs] * scale + bias_ref[2 * g + par, rs, :]
                    s = jnp.where(valid, s, NEG)
                    sink = sink_ref[h]
                    m = jnp.maximum(jnp.max(s, axis=-1, keepdims=True), sink)
                    p = jnp.exp(s - m)
                    den = jnp.sum(p, axis=-1, keepdims=True) + jnp.exp(sink - m)
                    ps.append((p / den).astype(BF16))
                p_all = jnp.concatenate(ps, axis=0)
                part = jnp.dot(p_all, v_ext, preferred_element_type=F32)
                out = part if out is None else out + part
            for a in range(pairs):
                o_ref[:, (pairs * g + a) * LANES:(pairs * g + a + 1) * LANES] = (
                    out[a * BLOCK:(a + 1) * BLOCK].astype(o_ref.dtype))


def swa_attention(qkv, sinks, rel_bias, *, seq):
    n = qkv.shape[0]
    nblk = n // BLOCK
    kcol = Q_W // KV_W
    bucket = jnp.asarray(_t5_bucket_table())
    smem = pl.BlockSpec(memory_space=pltpu.SMEM)
    return pl.pallas_call(
        functools.partial(_attn_kernel, blocks_per_seq=seq // BLOCK),
        grid=(nblk,),
        in_specs=[smem, smem,
                  pl.BlockSpec((BLOCK, 2 * BLOCK), lambda r: (0, 0)),
                  pl.BlockSpec((BLOCK, Q_W), lambda r: (r, 0)),
                  pl.BlockSpec((BLOCK, KV_W), lambda r: (r, kcol)),
                  pl.BlockSpec((BLOCK, KV_W), lambda r: (jnp.maximum(r - 1, 0), kcol)),
                  pl.BlockSpec((BLOCK, KV_W), lambda r: (r, kcol + 1)),
                  pl.BlockSpec((BLOCK, KV_W), lambda r: (jnp.maximum(r - 1, 0), kcol + 1))],
        out_specs=pl.BlockSpec((BLOCK, Q_W), lambda r: (r, 0)),
        out_shape=jax.ShapeDtypeStruct((n, Q_W), BF16),
        scratch_shapes=[pltpu.VMEM((2 * N_KV, (GROUP // 2) * BLOCK, 2 * BLOCK), F32)],
        compiler_params=_cparams(1),
        name="swa_attention",
    )(rel_bias, sinks, bucket, qkv, qkv, qkv, qkv, qkv)


CHUNK_T = SUBLANES


def _cmul(x, y):
    return x[0] * y[0] - x[1] * y[1], x[0] * y[1] + x[1] * y[0]


def _s5_abar(lr, li, ls):
    delta = jnp.exp(ls)
    mag = jnp.exp(lr * delta)
    ar = mag * jnp.cos(li * delta)
    ai = mag * jnp.sin(li * delta)
    nr = ar - 1.0
    ni = ai
    den = lr * lr + li * li
    fr = (nr * lr + ni * li) / den
    fi = (ni * lr - nr * li) / den
    return ar, ai, fr, fi


def _s5_prep_kernel(lr_ref, li_ref, ls_ref, lrc_ref, lic_ref, lsc_ref, br_ref, bi_ref, cr_ref,
                    bst_ref, kkw_ref, coef_ref):
    ns = STATES_PER_CTILE
    t = CHUNK_T
    ar, ai, fr, fi = _s5_abar(lr_ref[0], li_ref[0], ls_ref[0])
    is_re = lax.broadcasted_iota(jnp.int32, (1, 2 * ns), 1) < ns
    c_br = jnp.where(is_re, fr, fi)
    c_bi = jnp.where(is_re, -fi, fr)
    wb = c_br * br_ref[0] + c_bi * bi_ref[0]
    a1 = (ar[:, :ns], ai[:, :ns])

    def scale_cols(w):
        re, im = _cmul((w[:, :ns], w[:, ns:]), a1)
        return jnp.concatenate([re, im], axis=1)

    wbd = [wb]
    for _ in range(t - 1):
        wbd.append(scale_cols(wbd[-1]))
    for q in range(t):
        bst_ref[0, q * LANES:(q + 1) * LANES, :] = wbd[t - 1 - q].astype(BF16)

    is_re_row = lax.broadcasted_iota(jnp.int32, (2 * ns, LANES), 0) < ns
    cr = cr_ref[0]
    wc = jnp.where(is_re_row, cr, -cr)
    kmat = [jnp.dot(w, wc, preferred_element_type=F32, precision=lax.Precision.HIGHEST)
            for w in wbd]

    arc, aic, _, _ = _s5_abar(lrc_ref[0], lic_ref[0], lsc_ref[0])
    ac = (arc[:ns], aic[:ns])
    pc = [ac]
    for _ in range(t - 1):
        pc.append(_cmul(pc[-1], ac))
    wc_re, wc_im = wc[:ns], wc[ns:]
    zero_blk = jnp.zeros((LANES, LANES), BF16)
    for r in range(t):
        pr, half = divmod(r, 2)
        cols = slice(half * LANES, (half + 1) * LANES)
        for q in range(t):
            blk = kmat[r - q].astype(BF16) if q <= r else zero_blk
            kkw_ref[0, pr, q * LANES:(q + 1) * LANES, cols] = blk
        p_r, p_i = pc[r]
        kkw_ref[0, pr, t * LANES:t * LANES + ns, cols] = (p_r * wc_re + p_i * wc_im).astype(BF16)
        kkw_ref[0, pr, t * LANES + ns:, cols] = (p_r * wc_im - p_i * wc_re).astype(BF16)

    a2 = _cmul(a1, a1)
    a4 = _cmul(a2, a2)
    b1 = _cmul(a4, a4)
    b2 = _cmul(b1, b1)
    b3 = _cmul(b2, b1)
    b4 = _cmul(b2, b2)
    pw = [b1, b2, b3, b4, _cmul(b4, b1), _cmul(b4, b2), _cmul(b4, b3), _cmul(b4, b4)]
    rowi = lax.broadcasted_iota(jnp.int32, (SUBLANES, ns), 0)
    zero = jnp.zeros((SUBLANES, ns), F32)
    for part in range(2):
        coef_ref[0, 0 + part] = jnp.where(rowi >= 1, b1[part], zero)
        coef_ref[0, 2 + part] = jnp.where(rowi >= 2, b2[part], zero)
        coef_ref[0, 4 + part] = jnp.where(rowi >= 4, b4[part], zero)
        p = zero
        for rr in range(SUBLANES):
            p = jnp.where(rowi == rr, pw[rr][part], p)
        coef_ref[0, 6 + part] = p


def s5_prepare(lam_re, lam_im, log_step, b_re, b_im, c_re, c_im):
    g, p = lam_re.shape
    gc = b_re.shape[-1]
    gpt = GROUPS_PER_CTILE
    ct = g // gpt
    ns = STATES_PER_CTILE
    t = CHUNK_T
    same = jnp.arange(gpt)[:, None] == jnp.arange(gpt)[None, :]

    def cols(v):
        v = v.reshape(ct, 1, ns)
        return jnp.concatenate([v, v], axis=-1)

    def b_arr(b):
        b5 = jnp.transpose(b.reshape(ct, gpt, p, gc), (0, 3, 1, 2))[:, None]
        m = jnp.where(same[None, :, None, :, None], b5, 0.0)
        m = m.reshape(ct, gpt * gc, ns)
        return jnp.concatenate([m, m], axis=-1)

    def c_arr(c):
        c5 = jnp.transpose(c.reshape(ct, gpt, gc, p), (0, 1, 3, 2))[:, :, :, None]
        m = jnp.where(same[None, :, None, :, None], c5, 0.0)
        return m.reshape(ct, ns, gpt * gc)

    lr = cols(lam_re)
    li = cols(lam_im)
    ls = cols(jnp.broadcast_to(log_step[:, None], (g, p)))
    br = b_arr(b_re)
    bi = b_arr(b_im)
    cr = jnp.concatenate([c_arr(c_re), c_arr(c_im)], axis=1)
    vec = pl.BlockSpec((1, 1, 2 * ns), lambda c: (c, 0, 0))
    colv = pl.BlockSpec((1, 2 * ns, 1), lambda c: (c, 0, 0))
    as_col = lambda v: v.reshape(ct, 2 * ns, 1)
    kdim = t * LANES + 2 * ns
    return pl.pallas_call(
        _s5_prep_kernel,
        grid=(ct,),
        in_specs=[vec, vec, vec, colv, colv, colv,
                  pl.BlockSpec((1, LANES, 2 * ns), lambda c: (c, 0, 0)),
                  pl.BlockSpec((1, LANES, 2 * ns), lambda c: (c, 0, 0)),
                  pl.BlockSpec((1, 2 * ns, LANES), lambda c: (c, 0, 0))],
        out_specs=[pl.BlockSpec((1, t * LANES, 2 * ns), lambda c: (c, 0, 0)),
                   pl.BlockSpec((1, t // 2, kdim, 2 * LANES), lambda c: (c, 0, 0, 0)),
                   pl.BlockSpec((1, 8, SUBLANES, ns), lambda c: (c, 0, 0, 0))],
        out_shape=[jax.ShapeDtypeStruct((ct, t * LANES, 2 * ns), BF16),
                   jax.ShapeDtypeStruct((ct, t // 2, kdim, 2 * LANES), BF16),
                   jax.ShapeDtypeStruct((ct, 8, SUBLANES, ns), F32)],
        compiler_params=_cparams(1),
        name="s5_prepare",
    )(lr, li, ls, as_col(lr), as_col(li), as_col(ls), br, bi, cr)


def _s5_kernel(x_ref, rs_ref, g_ref, d_ref, bst_ref, kkw_ref, coef_ref, o_ref, h_ref, s_ref, y_ref, *, seq):
    ns = STATES_PER_CTILE
    lane_tiles = ns // LANES
    t = CHUNK_T
    nck = seq // t

    h = (x_ref[...] * rs_ref[...]) * g_ref[...]
    h_ref[0:t, :] = jnp.zeros((t, LANES), F32)
    h_ref[t:t + seq, :] = h

    def flat(row0):
        return jnp.concatenate([h_ref[pl.ds(row0 + q, nck, stride=t), :] for q in range(t)],
                               axis=1).astype(BF16)

    s_ref[...] = jnp.dot(flat(0), bst_ref[0], preferred_element_type=F32)

    def tile_body(k, carry):
        r0 = pl.multiple_of(k * SUBLANES, SUBLANES)
        new_carry = []
        for lt in range(lane_tiles):
            cs = slice(lt * LANES, (lt + 1) * LANES)
            ci = slice(ns + lt * LANES, ns + (lt + 1) * LANES)
            yr = s_ref[pl.ds(r0, SUBLANES), cs]
            yi = s_ref[pl.ds(r0, SUBLANES), ci]
            for step, shift in enumerate((1, 2, 4)):
                cr = coef_ref[0, 2 * step, :, cs]
                cim = coef_ref[0, 2 * step + 1, :, cs]
                sr = pltpu.roll(yr, shift, 0)
                si = pltpu.roll(yi, shift, 0)
                yr, yi = yr + (cr * sr - cim * si), yi + (cr * si + cim * sr)
            pr = coef_ref[0, 6, :, cs]
            pim = coef_ref[0, 7, :, cs]
            c_r, c_i = carry[2 * lt], carry[2 * lt + 1]
            xr = yr + (pr * c_r - pim * c_i)
            xi = yi + (pr * c_i + pim * c_r)
            s_ref[pl.ds(r0, SUBLANES), cs] = xr
            s_ref[pl.ds(r0, SUBLANES), ci] = xi
            new_carry.append(jnp.broadcast_to(xr[SUBLANES - 1:SUBLANES, :], (SUBLANES, LANES)))
            new_carry.append(jnp.broadcast_to(xi[SUBLANES - 1:SUBLANES, :], (SUBLANES, LANES)))
        return tuple(new_carry)

    zero = jnp.zeros((SUBLANES, LANES), F32)
    lax.fori_loop(0, nck // SUBLANES, tile_body, (zero,) * (2 * lane_tiles))

    lhs = jnp.concatenate([flat(t), s_ref[...].astype(BF16)], axis=1)
    for pr in range(t // 2):
        yp = jnp.dot(lhs, kkw_ref[0, pr], preferred_element_type=F32)
        y_ref[pl.ds(2 * pr, nck, stride=t), :] = yp[:, :LANES]
        y_ref[pl.ds(2 * pr + 1, nck, stride=t), :] = yp[:, LANES:]
    y = y_ref[...] + d_ref[...] * h
    o_ref[...] = jax.nn.gelu(y).astype(o_ref.dtype)


def s5_mix(x, rowscale, norm_g, d_skip, bst, kkw, coef, *, seq):
    n, d = x.shape
    ct = d // LANES
    ns = STATES_PER_CTILE
    t = CHUNK_T
    kdim = t * LANES + 2 * ns
    return pl.pallas_call(
        functools.partial(_s5_kernel, seq=seq),
        grid=(ct, n // seq),
        in_specs=[pl.BlockSpec((seq, LANES), lambda c, b: (b, c)),
                  pl.BlockSpec((seq, 1), lambda c, b: (b, 0)),
                  pl.BlockSpec((1, LANES), lambda c, b: (0, c)),
                  pl.BlockSpec((1, LANES), lambda c, b: (0, c)),
                  pl.BlockSpec((1, t * LANES, 2 * ns), lambda c, b: (c, 0, 0)),
                  pl.BlockSpec((1, t // 2, kdim, 2 * LANES), lambda c, b: (c, 0, 0, 0)),
                  pl.BlockSpec((1, 8, SUBLANES, ns), lambda c, b: (c, 0, 0, 0))],
        out_specs=pl.BlockSpec((seq, LANES), lambda c, b: (b, c)),
        out_shape=jax.ShapeDtypeStruct((n, d), BF16),
        scratch_shapes=[pltpu.VMEM((seq + t, LANES), F32),
                        pltpu.VMEM((seq // t, 2 * ns), F32),
                        pltpu.VMEM((seq, LANES), F32)],
        compiler_params=_cparams(2),
        name="s5_mix",
    )(x, rowscale, norm_g.reshape(1, d), d_skip.reshape(1, d), bst, kkw, coef)


def conv_gated_mlp(x, norm_g, w_up, conv_w, conv_b, w_down, layer, *, seq):
    h = rmsnorm(x, norm_g, layer, BF16)
    a = up_conv_gate(h, w_up, conv_w, conv_b, layer, seq=seq, tm=1024, tn=256)
    half = w_down.shape[1] // 2
    x = matmul(a, w_down, layer, resid=x, out_dtype=F32, tm=512, tn=512, k_block=0, k_size=half)
    return matmul(a, w_down, layer, resid=x, out_dtype=F32, tm=512, tn=512, k_block=1, k_size=half)


def kernel(x, attn_norm, w_qkv, w_o, sinks, rel_bias, ssm_norm, lambda_re, lambda_im, log_step, b_re, b_im, c_re, c_im, d_skip, w_glu, ffn_norm, w_up, conv_w, conv_b, w_down, final_norm):
    batch, seq, d = x.shape
    x = x.reshape(batch * seq, d)

    h = rmsnorm(x, attn_norm, 0, BF16)
    qkv = matmul(h, w_qkv, 0, out_dtype=BF16, tm=1024, tn=512)
    o = swa_attention(qkv, sinks[0], rel_bias, seq=seq)
    x = matmul(o, w_o, 0, resid=x, out_dtype=F32, tm=1024, tn=512)
    x = conv_gated_mlp(x, ffn_norm, w_up, conv_w, conv_b, w_down, 0, seq=seq)

    bst, kkw, coef = s5_prepare(lambda_re[0], lambda_im[0], log_step[0], b_re[0], b_im[0], c_re[0], c_im[0])
    y = s5_mix(x, rms_rowscale(x), ssm_norm[0], d_skip[0], bst, kkw, coef, seq=seq)
    x = glu_matmul(y, w_glu, 0, x, tm=1024, tn=256)
    x = conv_gated_mlp(x, ffn_norm, w_up, conv_w, conv_b, w_down, 1, seq=seq)

    return rmsnorm(x, final_norm.reshape(1, d), 0, F32).reshape(batch, seq, d)
```
